```python
import math
import jax, jax.numpy as jnp
from jax import lax
import numpy as np

D_MODEL = 2048
BATCH = 4
SEQ = 8192
DEPTH = 1
DEC_BATCH = 16
DEC_SEQ = 2048
PAST_LEN = 128

HEAD_DIM = 128
ATTN_Q_HEADS = 8
ATTN_KV_HEADS = 2
ATTN_GROUP = ATTN_Q_HEADS // ATTN_KV_HEADS
ATTN_WIDTH = ATTN_Q_HEADS * HEAD_DIM
ATTN_KV_WIDTH = ATTN_KV_HEADS * HEAD_DIM
RET_HEADS = 8
RET_WIDTH = RET_HEADS * HEAD_DIM
MIX_WIDTH = ATTN_WIDTH + RET_WIDTH
IN_WIDTH = ATTN_WIDTH + 2 * ATTN_KV_WIDTH + 4 * RET_WIDTH
WINDOW = 128
RET_CHUNK = 128
FFN_HIDDEN = ((8 * D_MODEL + 3 * 256 - 1) // (3 * 256)) * 256
DEEPNORM_ALPHA = (2.0 * DEPTH) ** 0.25
DEEPNORM_BETA = (8.0 * DEPTH) ** -0.25
LN_EPS = 1e-5
GN_EPS = 1e-5
NEG_INF = -1e30

kernel_name = "hybrid_bidir_swa_retention_encoder"


def _layer_norm(x, gain, bias):
    xf = x.astype(jnp.float32)
    mu = jnp.mean(xf, axis=-1, keepdims=True)
    var = jnp.mean(jnp.square(xf - mu), axis=-1, keepdims=True)
    y = (xf - mu) * lax.rsqrt(var + LN_EPS) * gain.astype(jnp.float32) + bias.astype(jnp.float32)
    return y.astype(x.dtype)


def _windowed_gqa_alibi_sink(q, k, v, sink):
    B, S, _ = q.shape
    W = WINDOW
    N = S // W
    qb = q.reshape(B, N, W, ATTN_KV_HEADS, ATTN_GROUP, HEAD_DIM)
    k = k.reshape(B, S, ATTN_KV_HEADS, HEAD_DIM)
    v = v.reshape(B, S, ATTN_KV_HEADS, HEAD_DIM)
    pad = ((0, 0), (W, W), (0, 0), (0, 0))
    kp = jnp.pad(k, pad).reshape(B, N + 2, W, ATTN_KV_HEADS, HEAD_DIM)
    vp = jnp.pad(v, pad).reshape(B, N + 2, W, ATTN_KV_HEADS, HEAD_DIM)
    kb = jnp.concatenate([kp[:, :-2], kp[:, 1:-1], kp[:, 2:]], axis=2)
    vb = jnp.concatenate([vp[:, :-2], vp[:, 1:-1], vp[:, 2:]], axis=2)
    s = jnp.einsum('bnqkgd,bnskd->bnkgqs', qb, kb).astype(jnp.float32) * (HEAD_DIM ** -0.5)
    qpos = jnp.arange(W)
    kpos = jnp.arange(3 * W) - W
    dist = jnp.abs(qpos[:, None] - kpos[None, :])
    key_global = jnp.arange(N)[:, None] * W + kpos[None, :]
    valid = (dist <= W)[None] & ((key_global >= 0) & (key_global < S))[:, None, :]
    slopes = (2.0 ** (-8.0 * jnp.arange(1, ATTN_Q_HEADS + 1, dtype=jnp.float32) / ATTN_Q_HEADS)
              ).reshape(ATTN_KV_HEADS, ATTN_GROUP)
    s = s - slopes[:, :, None, None] * dist.astype(jnp.float32)
    s = jnp.where(valid[None, :, None, None], s, NEG_INF)
    sink_l = sink.astype(jnp.float32).reshape(ATTN_KV_HEADS, ATTN_GROUP)[:, :, None, None]
    m = jnp.maximum(jnp.max(s, axis=-1, keepdims=True), sink_l)
    e = jnp.exp(s - m)
    p = e / (jnp.sum(e, axis=-1, keepdims=True) + jnp.exp(sink_l - m))
    o = jnp.einsum('bnkgqs,bnskd->bnqkgd', p.astype(v.dtype), vb)
    return o.reshape(B, S, ATTN_WIDTH)


def _retention_direction(q, k, v, log_gamma, include_diag):
    B, S, H, Dk = q.shape
    Dv = v.shape[-1]
    C = RET_CHUNK
    N = S // C
    qc = q.reshape(B, N, C, H, Dk)
    kc = k.reshape(B, N, C, H, Dk)
    vc = v.reshape(B, N, C, H, Dv)
    pos = jnp.arange(C, dtype=jnp.float32)
    diff = pos[:, None] - pos[None, :]
    keep = (diff >= 0) if include_diag else (diff > 0)
    decay_intra = jnp.where(keep[None], jnp.exp(log_gamma[:, None, None] * jnp.maximum(diff, 0.0)[None]), 0.0)
    scores = jnp.einsum('bnihd,bnjhd->bnhij', qc, kc) * decay_intra
    intra = jnp.einsum('bnhij,bnjhe->bnihe', scores, vc)
    k_decay = jnp.exp(log_gamma[None, :] * (C - 1.0 - pos)[:, None])
    chunk_kv = jnp.einsum('bnjhd,bnjhe->nbhde', kc * k_decay[:, :, None], vc)
    chunk_decay = jnp.exp(log_gamma * C)[:, None, None]

    def step(state, kv):
        return state * chunk_decay + kv, state

    _, past = lax.scan(step, jnp.zeros((B, H, Dk, Dv), chunk_kv.dtype), chunk_kv)
    q_decay = jnp.exp(log_gamma[None, :] * (pos + 1.0)[:, None])
    cross = jnp.einsum('bnihd,nbhde->bnihe', qc * q_decay[:, :, None], past)
    return (intra + cross).reshape(B, S, H, Dv)


def _bidirectional_retention(q, k, v, g, decay_fwd, decay_bwd, gn_gain):
    B, S, _ = q.shape
    q = q.reshape(B, S, RET_HEADS, HEAD_DIM)
    k = k.reshape(B, S, RET_HEADS, HEAD_DIM) * (HEAD_DIM ** -0.5)
    v = v.reshape(B, S, RET_HEADS, HEAD_DIM)
    lg_f = jax.nn.log_sigmoid(decay_fwd.astype(jnp.float32))
    lg_b = jax.nn.log_sigmoid(decay_bwd.astype(jnp.float32))
    fwd = _retention_direction(q, k, v, lg_f, True)
    bwd = jnp.flip(_retention_direction(jnp.flip(q, 1), jnp.flip(k, 1), jnp.flip(v, 1), lg_b, False), 1)
    o = (fwd + bwd).astype(jnp.float32)
    mu = jnp.mean(o, axis=-1, keepdims=True)
    var = jnp.mean(jnp.square(o - mu), axis=-1, keepdims=True)
    o = ((o - mu) * lax.rsqrt(var + GN_EPS)).reshape(B, S, RET_WIDTH) * gn_gain.astype(jnp.float32)
    return (jax.nn.silu(g.astype(jnp.float32)) * o).astype(g.dtype)


def _trunk(x, w_in, attn_sink, ret_decay_fwd, ret_decay_bwd, ret_gn_gain, w_out,
           ln1_gain, ln1_bias, w_ffn_in, w_ffn_out, ln2_gain, ln2_bias):
    for l in range(DEPTH):
        proj = x @ w_in[l]
        o = 0
        q_a = proj[..., o:o + ATTN_WIDTH]; o += ATTN_WIDTH
        k_a = proj[..., o:o + ATTN_KV_WIDTH]; o += ATTN_KV_WIDTH
        v_a = proj[..., o:o + ATTN_KV_WIDTH]; o += ATTN_KV_WIDTH
        q_r = proj[..., o:o + RET_WIDTH]; o += RET_WIDTH
        k_r = proj[..., o:o + RET_WIDTH]; o += RET_WIDTH
        v_r = proj[..., o:o + RET_WIDTH]; o += RET_WIDTH
        g_r = proj[..., o:o + RET_WIDTH]
        attn = _windowed_gqa_alibi_sink(q_a, k_a, v_a, attn_sink[l])
        ret = _bidirectional_retention(q_r, k_r, v_r, g_r, ret_decay_fwd[l], ret_decay_bwd[l], ret_gn_gain[l])
        mix = jnp.concatenate([attn.astype(x.dtype), ret.astype(x.dtype)], axis=-1) @ w_out[l]
        x = _layer_norm(DEEPNORM_ALPHA * x + mix, ln1_gain[l], ln1_bias[l])
        gu = x @ w_ffn_in[l]
        ffn = (jax.nn.silu(gu[..., :FFN_HIDDEN]) * gu[..., FFN_HIDDEN:]) @ w_ffn_out[l]
        x = _layer_norm(DEEPNORM_ALPHA * x + ffn, ln2_gain[l], ln2_bias[l])
    return x


def setup_inputs(seed: int = 0) -> dict:
    key = jax.random.key(seed)
    ks = jax.random.split(key, 16)
    f32 = jnp.float32
    x_prompt = jax.random.normal(ks[0], (BATCH, SEQ, D_MODEL), f32)
    x_sample = jax.random.normal(ks[1], (DEC_BATCH, DEC_SEQ, D_MODEL), f32)
    col_scale = jnp.concatenate([
        jnp.ones((ATTN_WIDTH + ATTN_KV_WIDTH,), f32),
        jnp.full((ATTN_KV_WIDTH,), DEEPNORM_BETA, f32),
        jnp.ones((2 * RET_WIDTH,), f32),
        jnp.full((RET_WIDTH,), DEEPNORM_BETA, f32),
        jnp.ones((RET_WIDTH,), f32)])
    w_in = jax.random.normal(ks[2], (DEPTH, D_MODEL, IN_WIDTH), f32) * (D_MODEL ** -0.5) * col_scale
    attn_sink = 0.5 * jax.random.normal(ks[3], (DEPTH, ATTN_Q_HEADS), f32)
    base = jnp.log(2.0 ** (5.0 + jnp.arange(RET_HEADS, dtype=f32)) - 1.0)
    ret_decay_fwd = base[None] + 0.1 * jax.random.normal(ks[4], (DEPTH, RET_HEADS), f32)
    ret_decay_bwd = base[None] + 0.1 * jax.random.normal(ks[5], (DEPTH, RET_HEADS), f32)
    ret_gn_gain = 1.0 + 0.02 * jax.random.normal(ks[6], (DEPTH, RET_WIDTH), f32)
    w_out = jax.random.normal(ks[7], (DEPTH, MIX_WIDTH, D_MODEL), f32) * (MIX_WIDTH ** -0.5) * DEEPNORM_BETA
    ln1_gain = 1.0 + 0.02 * jax.random.normal(ks[8], (DEPTH, D_MODEL), f32)
    ln1_bias = 0.01 * jax.random.normal(ks[9], (DEPTH, D_MODEL), f32)
    w_ffn_in = jax.random.normal(ks[10], (DEPTH, D_MODEL, 2 * FFN_HIDDEN), f32) * (D_MODEL ** -0.5) * DEEPNORM_BETA
    w_ffn_out = jax.random.normal(ks[11], (DEPTH, FFN_HIDDEN, D_MODEL), f32) * (FFN_HIDDEN ** -0.5) * DEEPNORM_BETA
    ln2_gain = 1.0 + 0.02 * jax.random.normal(ks[12], (DEPTH, D_MODEL), f32)
    ln2_bias = 0.01 * jax.random.normal(ks[13], (DEPTH, D_MODEL), f32)
    return {"x_prompt": x_prompt, "x_sample": x_sample, "w_in": w_in, "attn_sink": attn_sink,
            "ret_decay_fwd": ret_decay_fwd, "ret_decay_bwd": ret_decay_bwd, "ret_gn_gain": ret_gn_gain,
            "w_out": w_out, "ln1_gain": ln1_gain, "ln1_bias": ln1_bias, "w_ffn_in": w_ffn_in,
            "w_ffn_out": w_ffn_out, "ln2_gain": ln2_gain, "ln2_bias": ln2_bias}


def reference(x_prompt, x_sample, w_in, attn_sink, ret_decay_fwd, ret_decay_bwd, ret_gn_gain, w_out,
              ln1_gain, ln1_bias, w_ffn_in, w_ffn_out, ln2_gain, ln2_bias):
    y_prompt = _trunk(x_prompt, w_in, attn_sink, ret_decay_fwd, ret_decay_bwd, ret_gn_gain, w_out,
                      ln1_gain, ln1_bias, w_ffn_in, w_ffn_out, ln2_gain, ln2_bias)
    y_sample = _trunk(x_sample, w_in, attn_sink, ret_decay_fwd, ret_decay_bwd, ret_gn_gain, w_out,
                      ln1_gain, ln1_bias, w_ffn_in, w_ffn_out, ln2_gain, ln2_bias)
    return (y_prompt, y_sample)
```

```python
import functools

import jax
import jax.numpy as jnp
from jax import lax
from jax.experimental import pallas as pl
from jax.experimental.pallas import tpu as pltpu

F32 = jnp.float32
BF16 = jnp.bfloat16

D_MODEL = 2048
HEAD_DIM = 128
ATTN_Q_HEADS = 8
ATTN_KV_HEADS = 2
ATTN_GROUP = ATTN_Q_HEADS // ATTN_KV_HEADS
ATTN_WIDTH = ATTN_Q_HEADS * HEAD_DIM
RET_HEADS = 8
RET_WIDTH = RET_HEADS * HEAD_DIM
IN_WIDTH = 5632
FFN_HIDDEN = 5632
WINDOW = 128
DEEPNORM_ALPHA = 2.0 ** 0.25
LN_EPS = 1e-5
GN_EPS = 1e-5
NEG_INF = -1e30
QK_SCALE = HEAD_DIM ** -0.5

N_SLABS = IN_WIDTH // HEAD_DIM
SLAB_QA = 0
SLAB_KA = SLAB_QA + ATTN_Q_HEADS
SLAB_VA = SLAB_KA + ATTN_KV_HEADS
SLAB_QR = SLAB_VA + ATTN_KV_HEADS
SLAB_KR = SLAB_QR + RET_HEADS
SLAB_VR = SLAB_KR + RET_HEADS
SLAB_GR = SLAB_VR + RET_HEADS

VMEM_LIMIT = 60 * 1024 * 1024

PROJ_BM = 512
PROJ_NC = 512
ATTN_TQ = 512
RET_CHUNK = 128
MIX_BM = 512
FFN_BM = 512
FFN_BH = 512


def _params(sem):
    return pltpu.CompilerParams(dimension_semantics=sem, vmem_limit_bytes=VMEM_LIMIT)


def _layer_norm_rows(z, gain, bias):
    mu = jnp.mean(z, axis=-1, keepdims=True)
    zc = z - mu
    var = jnp.mean(zc * zc, axis=-1, keepdims=True)
    return zc * lax.rsqrt(var + LN_EPS) * gain + bias


def _proj_kernel(x_ref, w_ref, o_ref):
    xb = x_ref[...].astype(BF16)
    slabs_per_chunk = PROJ_NC // HEAD_DIM
    for c in range(IN_WIDTH // PROJ_NC):
        acc = jnp.dot(xb, w_ref[:, c * PROJ_NC:(c + 1) * PROJ_NC], preferred_element_type=F32)
        for s in range(slabs_per_chunk):
            o_ref[c * slabs_per_chunk + s] = acc[:, s * HEAD_DIM:(s + 1) * HEAD_DIM].astype(BF16)


def _proj(x2, w_in):
    rows = x2.shape[0]
    return pl.pallas_call(
        _proj_kernel,
        grid=(rows // PROJ_BM,),
        in_specs=[
            pl.BlockSpec((PROJ_BM, D_MODEL), lambda i: (i, 0)),
            pl.BlockSpec((D_MODEL, IN_WIDTH), lambda i: (0, 0), pipeline_mode=pl.Buffered(1)),
        ],
        out_specs=pl.BlockSpec((N_SLABS, PROJ_BM, HEAD_DIM), lambda i: (0, i, 0)),
        out_shape=jax.ShapeDtypeStruct((N_SLABS, rows, HEAD_DIM), BF16),
        compiler_params=_params(("arbitrary",)),
        name="proj",
    )(x2, w_in)


def _attn_kernel(sink_ref, q_ref, kp_ref, kc_ref, kn_ref, vp_ref, vc_ref, vn_ref, o_ref,
                 bias_ref, kfull_ref, vfull_ref, *, n_tiles):
    b, kv, t = pl.program_id(0), pl.program_id(1), pl.program_id(2)
    W = WINDOW
    nsub = ATTN_TQ // W

    @pl.when((b == 0) & (kv == 0) & (t == 0))
    def _():
        qi = lax.broadcasted_iota(jnp.int32, (W, 3 * W), 0)
        kj = lax.broadcasted_iota(jnp.int32, (W, 3 * W), 1) - W
        dist = jnp.abs(qi - kj)
        distf = dist.astype(F32)
        inwin = dist <= W
        for h in range(ATTN_Q_HEADS):
            slope = 2.0 ** (-8.0 * (h + 1) / ATTN_Q_HEADS)
            base = jnp.where(inwin, -slope * distf, NEG_INF)
            bias_ref[0, h] = base
            bias_ref[1, h] = jnp.where(kj >= 0, base, NEG_INF)
            bias_ref[2, h] = jnp.where(kj < W, base, NEG_INF)

    kfull_ref[0:W] = kp_ref[0]
    kfull_ref[W:W + ATTN_TQ] = kc_ref[0]
    kfull_ref[W + ATTN_TQ:] = kn_ref[0]
    vfull_ref[0:W] = vp_ref[0]
    vfull_ref[W:W + ATTN_TQ] = vc_ref[0]
    vfull_ref[W + ATTN_TQ:] = vn_ref[0]

    for i in range(nsub):
        if i == 0:
            variant = jnp.where(t == 0, 1, 0)
        elif i == nsub - 1:
            variant = jnp.where(t == n_tiles - 1, 2, 0)
        else:
            variant = 0
        q4 = q_ref[:, i * W:(i + 1) * W, :].reshape(ATTN_GROUP * W, HEAD_DIM)
        kb = kfull_ref[i * W:(i + 3) * W]
        vb = vfull_ref[i * W:(i + 3) * W]
        s = lax.dot_general(q4, kb, (((1,), (1,)), ((), ())), preferred_element_type=F32)
        for g in range(ATTN_GROUP):
            h = kv * ATTN_GROUP + g
            sink = sink_ref[h]
            sg = s[g * W:(g + 1) * W] * QK_SCALE + bias_ref[variant, h]
            m = jnp.maximum(jnp.max(sg, axis=-1, keepdims=True), sink)
            e = jnp.exp(sg - m)
            den = jnp.sum(e, axis=-1, keepdims=True) + jnp.exp(sink - m)
            og = jnp.dot(e.astype(BF16), vb, preferred_element_type=F32) / den
            o_ref[i * W:(i + 1) * W, g * HEAD_DIM:(g + 1) * HEAD_DIM] = og.astype(BF16)


def _attn(proj, sink, batch, seq):
    rows = batch * seq
    n_tiles = seq // ATTN_TQ
    nblk = seq // WINDOW
    sub = ATTN_TQ // WINDOW
    assert nblk >= 2 and sub >= 2

    def cur(slab0):
        return pl.BlockSpec((1, ATTN_TQ, HEAD_DIM), lambda b, k, t: (slab0 + k, b * n_tiles + t, 0))

    def prev(slab0):
        return pl.BlockSpec((1, WINDOW, HEAD_DIM),
                            lambda b, k, t: (slab0 + k, b * nblk + jnp.maximum(t * sub - 1, 0), 0))

    def nxt(slab0):
        return pl.BlockSpec((1, WINDOW, HEAD_DIM),
                            lambda b, k, t: (slab0 + k, b * nblk + jnp.minimum((t + 1) * sub, nblk - 1), 0))

    return pl.pallas_call(
        functools.partial(_attn_kernel, n_tiles=n_tiles),
        grid=(batch, ATTN_KV_HEADS, n_tiles),
        in_specs=[
            pl.BlockSpec(memory_space=pltpu.SMEM),
            pl.BlockSpec((ATTN_GROUP, ATTN_TQ, HEAD_DIM), lambda b, k, t: (k, b * n_tiles + t, 0)),
            prev(SLAB_KA), cur(SLAB_KA), nxt(SLAB_KA),
            prev(SLAB_VA), cur(SLAB_VA), nxt(SLAB_VA),
        ],
        out_specs=pl.BlockSpec((ATTN_TQ, ATTN_GROUP * HEAD_DIM), lambda b, k, t: (b * n_tiles + t, k)),
        out_shape=jax.ShapeDtypeStruct((rows, ATTN_WIDTH), BF16),
        scratch_shapes=[
            pltpu.VMEM((3, ATTN_Q_HEADS, WINDOW, 3 * WINDOW), F32),
            pltpu.VMEM((ATTN_TQ + 2 * WINDOW, HEAD_DIM), BF16),
            pltpu.VMEM((ATTN_TQ + 2 * WINDOW, HEAD_DIM), BF16),
        ],
        compiler_params=_params(("arbitrary", "arbitrary", "arbitrary")),
        name="attn",
    )(sink, proj, proj, proj, proj, proj, proj, proj)


def _ret_kernel(lgf_ref, lgb_ref, gain_ref, q_ref, k_ref, v_ref, g_ref, o_ref, st_ref, *, n_chunks):
    h = pl.program_id(1)
    C = RET_CHUNK
    lgf = lgf_ref[h]
    lgb = lgb_ref[h]

    ri = lax.broadcasted_iota(jnp.int32, (C, C), 0).astype(F32)
    ci = lax.broadcasted_iota(jnp.int32, (C, C), 1).astype(F32)
    diff = ri - ci
    dmat = jnp.where(diff >= 0.0,
                     jnp.exp(lgf * jnp.maximum(diff, 0.0)),
                     jnp.exp(lgb * jnp.maximum(-diff, 0.0))) * QK_SCALE
    pos = lax.broadcasted_iota(jnp.int32, (C, HEAD_DIM), 0).astype(F32)
    q_dec_f = jnp.exp(lgf * (pos + 1.0))
    q_dec_b = jnp.exp(lgb * (C - pos))
    k_dec_f = jnp.exp(lgf * (C - 1.0 - pos)) * QK_SCALE
    k_dec_b = jnp.exp(lgb * pos) * QK_SCALE
    chunk_dec_f = jnp.exp(jnp.full((HEAD_DIM, HEAD_DIM), lgf * C, F32))
    chunk_dec_b = jnp.exp(jnp.full((HEAD_DIM, HEAD_DIM), lgb * C, F32))

    def kv_outer(kc, vc, dec):
        kd = (kc.astype(F32) * dec).astype(BF16)
        return lax.dot_general(kd, vc, (((0,), (0,)), ((), ())), preferred_element_type=F32)

    def scan_body(n, carry):
        sf, sb = carry
        m = n_chunks - 1 - n
        st_ref[n, 0:HEAD_DIM, :] = sf.astype(BF16)
        st_ref[m, HEAD_DIM:, :] = sb.astype(BF16)
        rf = pl.ds(pl.multiple_of(n * C, C), C)
        rb = pl.ds(pl.multiple_of(m * C, C), C)
        sf = chunk_dec_f * sf + kv_outer(k_ref[0, rf, :], v_ref[0, rf, :], k_dec_f)
        sb = chunk_dec_b * sb + kv_outer(k_ref[0, rb, :], v_ref[0, rb, :], k_dec_b)
        return sf, sb

    zero = jnp.zeros((HEAD_DIM, HEAD_DIM), F32)
    lax.fori_loop(0, n_chunks, scan_body, (zero, zero))

    gain = gain_ref[0]

    def out_body(n, carry):
        r = pl.ds(pl.multiple_of(n * C, C), C)
        qc = q_ref[0, r, :]
        kc = k_ref[0, r, :]
        vc = v_ref[0, r, :]
        qf = qc.astype(F32)
        s = lax.dot_general(qc, kc, (((1,), (1,)), ((), ())), preferred_element_type=F32) * dmat
        qcat = jnp.concatenate([(qf * q_dec_f).astype(BF16), (qf * q_dec_b).astype(BF16)], axis=-1)
        o = (jnp.dot(s.astype(BF16), vc, preferred_element_type=F32)
             + jnp.dot(qcat, st_ref[n], preferred_element_type=F32))
        mu = jnp.mean(o, axis=-1, keepdims=True)
        oc = o - mu
        var = jnp.mean(oc * oc, axis=-1, keepdims=True)
        on = oc * lax.rsqrt(var + GN_EPS) * gain
        gt = g_ref[0, r, :].astype(F32)
        o_ref[r, :] = (gt / (1.0 + jnp.exp(-gt)) * on).astype(BF16)
        return carry

    lax.fori_loop(0, n_chunks, out_body, 0)


def _ret(proj, lgf, lgb, gn_gain, batch, seq):
    rows = batch * seq
    n_chunks = seq // RET_CHUNK

    def slab(slab0):
        return pl.BlockSpec((1, seq, HEAD_DIM), lambda b, h: (slab0 + h, b, 0))

    return pl.pallas_call(
        functools.partial(_ret_kernel, n_chunks=n_chunks),
        grid=(batch, RET_HEADS),
        in_specs=[
            pl.BlockSpec(memory_space=pltpu.SMEM),
            pl.BlockSpec(memory_space=pltpu.SMEM),
            pl.BlockSpec((1, 1, HEAD_DIM), lambda b, h: (h, 0, 0)),
            slab(SLAB_QR), slab(SLAB_KR), slab(SLAB_VR), slab(SLAB_GR),
        ],
        out_specs=pl.BlockSpec((seq, HEAD_DIM), lambda b, h: (b, h)),
        out_shape=jax.ShapeDtypeStruct((rows, RET_WIDTH), BF16),
        scratch_shapes=[pltpu.VMEM((n_chunks, 2 * HEAD_DIM, HEAD_DIM), BF16)],
        compiler_params=_params(("arbitrary", "arbitrary")),
        name="ret",
    )(lgf, lgb, gn_gain.reshape(RET_HEADS, 1, HEAD_DIM), proj, proj, proj, proj)


def _mix_kernel(a_ref, r_ref, x_ref, w_ref, g_ref, b_ref, o_ref):
    mix = (jnp.dot(a_ref[...], w_ref[0:ATTN_WIDTH, :], preferred_element_type=F32)
           + jnp.dot(r_ref[...], w_ref[ATTN_WIDTH:, :], preferred_element_type=F32))
    z = DEEPNORM_ALPHA * x_ref[...] + mix
    o_ref[...] = _layer_norm_rows(z, g_ref[...], b_ref[...])


def _mix_ln(attn, ret, x2, w_out, gain, bias):
    rows = x2.shape[0]
    return pl.pallas_call(
        _mix_kernel,
        grid=(rows // MIX_BM,),
        in_specs=[
            pl.BlockSpec((MIX_BM, ATTN_WIDTH), lambda i: (i, 0)),
            pl.BlockSpec((MIX_BM, RET_WIDTH), lambda i: (i, 0)),
            pl.BlockSpec((MIX_BM, D_MODEL), lambda i: (i, 0)),
            pl.BlockSpec((D_MODEL, D_MODEL), lambda i: (0, 0), pipeline_mode=pl.Buffered(1)),
            pl.BlockSpec((1, D_MODEL), lambda i: (0, 0)),
            pl.BlockSpec((1, D_MODEL), lambda i: (0, 0)),
        ],
        out_specs=pl.BlockSpec((MIX_BM, D_MODEL), lambda i: (i, 0)),
        out_shape=jax.ShapeDtypeStruct((rows, D_MODEL), F32),
        compiler_params=_params(("arbitrary",)),
        name="mix_ln",
    )(attn, ret, x2, w_out, gain, bias)


def _ffn_kernel(h_ref, wg_ref, wu_ref, wo_ref, g_ref, b_ref, o_ref, hb_ref):
    j = pl.program_id(1)

    @pl.when(j == 0)
    def _():
        hb_ref[...] = h_ref[...].astype(BF16)

    hb = hb_ref[...]
    gate = jnp.dot(hb, wg_ref[...], preferred_element_type=F32)
    up = jnp.dot(hb, wu_ref[...], preferred_element_type=F32)
    act = (gate / (1.0 + jnp.exp(-gate)) * up).astype(BF16)
    part = jnp.dot(act, wo_ref[...], preferred_element_type=F32)

    @pl.when(j == 0)
    def _():
        o_ref[...] = part

    @pl.when(j > 0)
    def _():
        o_ref[...] += part

    @pl.when(j == pl.num_programs(1) - 1)
    def _():
        z = DEEPNORM_ALPHA * h_ref[...] + o_ref[...]
        o_ref[...] = _layer_norm_rows(z, g_ref[...], b_ref[...])


def _ffn_ln(h, w_ffn_in, w_ffn_out, gain, bias):
    rows = h.shape[0]
    nh = FFN_HIDDEN // FFN_BH
    return pl.pallas_call(
        _ffn_kernel,
        grid=(rows // FFN_BM, nh),
        in_specs=[
            pl.BlockSpec((FFN_BM, D_MODEL), lambda i, j: (i, 0)),
            pl.BlockSpec((D_MODEL, FFN_BH), lambda i, j: (0, j)),
            pl.BlockSpec((D_MODEL, FFN_BH), lambda i, j: (0, nh + j)),
            pl.BlockSpec((FFN_BH, D_MODEL), lambda i, j: (j, 0)),
            pl.BlockSpec((1, D_MODEL), lambda i, j: (0, 0)),
            pl.BlockSpec((1, D_MODEL), lambda i, j: (0, 0)),
        ],
        out_specs=pl.BlockSpec((FFN_BM, D_MODEL), lambda i, j: (i, 0)),
        out_shape=jax.ShapeDtypeStruct((rows, D_MODEL), F32),
        scratch_shapes=[pltpu.VMEM((FFN_BM, D_MODEL), BF16)],
        compiler_params=_params(("arbitrary", "arbitrary")),
        name="ffn_ln",
    )(h, w_ffn_in, w_ffn_in, w_ffn_out, gain, bias)


def _layer(x, w_in, sink, lgf, lgb, gn_gain, w_out, ln1_g, ln1_b, w_ffn_in, w_ffn_out, ln2_g, ln2_b):
    batch, seq, _ = x.shape
    x2 = x.reshape(batch * seq, D_MODEL)
    proj = _proj(x2, w_in)
    attn = _attn(proj, sink, batch, seq)
    ret = _ret(proj, lgf, lgb, gn_gain, batch, seq)
    h = _mix_ln(attn, ret, x2, w_out, ln1_g, ln1_b)
    y = _ffn_ln(h, w_ffn_in, w_ffn_out, ln2_g, ln2_b)
    return y.reshape(batch, seq, D_MODEL)


def kernel(x_prompt, x_sample, w_in, attn_sink, ret_decay_fwd, ret_decay_bwd, ret_gn_gain, w_out,
           ln1_gain, ln1_bias, w_ffn_in, w_ffn_out, ln2_gain, ln2_bias):
    l = 0
    args = (
        w_in[l].astype(BF16),
        attn_sink[l].astype(F32),
        jax.nn.log_sigmoid(ret_decay_fwd[l].astype(F32)),
        jax.nn.log_sigmoid(ret_decay_bwd[l].astype(F32)),
        ret_gn_gain[l].astype(F32),
        w_out[l].astype(BF16),
        ln1_gain[l].astype(F32).reshape(1, D_MODEL),
        ln1_bias[l].astype(F32).reshape(1, D_MODEL),
        w_ffn_in[l].astype(BF16),
        w_ffn_out[l].astype(BF16),
        ln2_gain[l].astype(F32).reshape(1, D_MODEL),
        ln2_bias[l].astype(F32).reshape(1, D_MODEL),
    )
    return (_layer(x_prompt, *args), _layer(x_sample, *args))
```

```python
import functools

import jax
import jax.numpy as jnp
from jax import lax
from jax.experimental import pallas as pl
from jax.experimental.pallas import tpu as pltpu

F32 = jnp.float32
BF16 = jnp.bfloat16

D_MODEL = 2048
HEAD_DIM = 128
ATTN_Q_HEADS = 8
ATTN_KV_HEADS = 2
ATTN_GROUP = ATTN_Q_HEADS // ATTN_KV_HEADS
ATTN_WIDTH = ATTN_Q_HEADS * HEAD_DIM
RET_HEADS = 8
RET_WIDTH = RET_HEADS * HEAD_DIM
IN_WIDTH = 5632
FFN_HIDDEN = 5632
WINDOW = 128
DEEPNORM_ALPHA = 2.0 ** 0.25
LN_EPS = 1e-5
GN_EPS = 1e-5
NEG_INF = -1e30
QK_SCALE = HEAD_DIM ** -0.5

N_SLABS = IN_WIDTH // HEAD_DIM
SLAB_QA = 0
SLAB_KA = SLAB_QA + ATTN_Q_HEADS
SLAB_VA = SLAB_KA + ATTN_KV_HEADS
SLAB_QR = SLAB_VA + ATTN_KV_HEADS
SLAB_KR = SLAB_QR + RET_HEADS
SLAB_VR = SLAB_KR + RET_HEADS
SLAB_GR = SLAB_VR + RET_HEADS

VMEM_LIMIT = 60 * 1024 * 1024

PROJ_BM = 512
PROJ_NC = 512
ATTN_TQ = 512
RET_CHUNK = 128
RET_UNROLL = 8
MIX_BM = 512
MIX_SUB = 256
FFN_BM = 512
FFN_BH = 512
FFN_NC = 512


def _params(sem):
    return pltpu.CompilerParams(dimension_semantics=sem, vmem_limit_bytes=VMEM_LIMIT)


def _layer_norm_rows(z, gain, bias):
    mu = jnp.mean(z, axis=-1, keepdims=True)
    zc = z - mu
    var = jnp.mean(zc * zc, axis=-1, keepdims=True)
    return zc * lax.rsqrt(var + LN_EPS) * gain + bias


def _proj_kernel(x_ref, w_ref, o_ref):
    xb = x_ref[...].astype(BF16)
    slabs_per_chunk = PROJ_NC // HEAD_DIM
    for c in range(IN_WIDTH // PROJ_NC):
        acc = jnp.dot(xb, w_ref[:, c * PROJ_NC:(c + 1) * PROJ_NC], preferred_element_type=F32)
        for s in range(slabs_per_chunk):
            o_ref[c * slabs_per_chunk + s] = acc[:, s * HEAD_DIM:(s + 1) * HEAD_DIM].astype(BF16)


def _proj(x2, w_in):
    rows = x2.shape[0]
    return pl.pallas_call(
        _proj_kernel,
        grid=(rows // PROJ_BM,),
        in_specs=[
            pl.BlockSpec((PROJ_BM, D_MODEL), lambda i: (i, 0)),
            pl.BlockSpec((D_MODEL, IN_WIDTH), lambda i: (0, 0), pipeline_mode=pl.Buffered(1)),
        ],
        out_specs=pl.BlockSpec((N_SLABS, PROJ_BM, HEAD_DIM), lambda i: (0, i, 0)),
        out_shape=jax.ShapeDtypeStruct((N_SLABS, rows, HEAD_DIM), BF16),
        compiler_params=_params(("arbitrary",)),
        name="proj",
    )(x2, w_in)


def _attn_kernel(sink_ref, q_ref, kp_ref, kc_ref, kn_ref, vp_ref, vc_ref, vn_ref, o_ref,
                 bias_ref, kfull_ref, vfull_ref, *, n_tiles):
    b, kv, t = pl.program_id(0), pl.program_id(1), pl.program_id(2)
    W = WINDOW
    nsub = ATTN_TQ // W

    @pl.when((b == 0) & (kv == 0) & (t == 0))
    def _():
        qi = lax.broadcasted_iota(jnp.int32, (W, 3 * W), 0)
        kj = lax.broadcasted_iota(jnp.int32, (W, 3 * W), 1) - W
        dist = jnp.abs(qi - kj)
        distf = dist.astype(F32)
        inwin = dist <= W
        for h in range(ATTN_Q_HEADS):
            slope = 2.0 ** (-8.0 * (h + 1) / ATTN_Q_HEADS)
            base = jnp.where(inwin, -slope * distf, NEG_INF)
            bias_ref[0, h] = base
            bias_ref[1, h] = jnp.where(kj >= 0, base, NEG_INF)
            bias_ref[2, h] = jnp.where(kj < W, base, NEG_INF)

    kfull_ref[0:W] = kp_ref[0]
    kfull_ref[W:W + ATTN_TQ] = kc_ref[0]
    kfull_ref[W + ATTN_TQ:] = kn_ref[0]
    vfull_ref[0:W] = vp_ref[0]
    vfull_ref[W:W + ATTN_TQ] = vc_ref[0]
    vfull_ref[W + ATTN_TQ:] = vn_ref[0]

    for i in range(nsub):
        if i == 0:
            variant = jnp.where(t == 0, 1, 0)
        elif i == nsub - 1:
            variant = jnp.where(t == n_tiles - 1, 2, 0)
        else:
            variant = 0
        q4 = q_ref[:, i * W:(i + 1) * W, :].reshape(ATTN_GROUP * W, HEAD_DIM)
        kb = kfull_ref[i * W:(i + 3) * W]
        vb = vfull_ref[i * W:(i + 3) * W]
        s = lax.dot_general(q4, kb, (((1,), (1,)), ((), ())), preferred_element_type=F32)
        for g in range(ATTN_GROUP):
            h = kv * ATTN_GROUP + g
            sink = sink_ref[h]
            sg = s[g * W:(g + 1) * W] * QK_SCALE + bias_ref[variant, h]
            m = jnp.maximum(jnp.max(sg, axis=-1, keepdims=True), sink)
            e = jnp.exp(sg - m)
            den = jnp.sum(e, axis=-1, keepdims=True) + jnp.exp(sink - m)
            og = jnp.dot(e.astype(BF16), vb, preferred_element_type=F32) / den
            o_ref[i * W:(i + 1) * W, g * HEAD_DIM:(g + 1) * HEAD_DIM] = og.astype(BF16)


def _attn(proj, sink, batch, seq):
    rows = batch * seq
    n_tiles = seq // ATTN_TQ
    nblk = seq // WINDOW
    sub = ATTN_TQ // WINDOW
    assert nblk >= 2 and sub >= 2

    def cur(slab0):
        return pl.BlockSpec((1, ATTN_TQ, HEAD_DIM), lambda b, k, t: (slab0 + k, b * n_tiles + t, 0))

    def prev(slab0):
        return pl.BlockSpec((1, WINDOW, HEAD_DIM),
                            lambda b, k, t: (slab0 + k, b * nblk + jnp.maximum(t * sub - 1, 0), 0))

    def nxt(slab0):
        return pl.BlockSpec((1, WINDOW, HEAD_DIM),
                            lambda b, k, t: (slab0 + k, b * nblk + jnp.minimum((t + 1) * sub, nblk - 1), 0))

    return pl.pallas_call(
        functools.partial(_attn_kernel, n_tiles=n_tiles),
        grid=(batch, ATTN_KV_HEADS, n_tiles),
        in_specs=[
            pl.BlockSpec(memory_space=pltpu.SMEM),
            pl.BlockSpec((ATTN_GROUP, ATTN_TQ, HEAD_DIM), lambda b, k, t: (k, b * n_tiles + t, 0)),
            prev(SLAB_KA), cur(SLAB_KA), nxt(SLAB_KA),
            prev(SLAB_VA), cur(SLAB_VA), nxt(SLAB_VA),
        ],
        out_specs=pl.BlockSpec((ATTN_TQ, ATTN_GROUP * HEAD_DIM), lambda b, k, t: (b * n_tiles + t, k)),
        out_shape=jax.ShapeDtypeStruct((rows, ATTN_WIDTH), BF16),
        scratch_shapes=[
            pltpu.VMEM((3, ATTN_Q_HEADS, WINDOW, 3 * WINDOW), F32),
            pltpu.VMEM((ATTN_TQ + 2 * WINDOW, HEAD_DIM), BF16),
            pltpu.VMEM((ATTN_TQ + 2 * WINDOW, HEAD_DIM), BF16),
        ],
        compiler_params=_params(("arbitrary", "arbitrary", "arbitrary")),
        name="attn",
    )(sink, proj, proj, proj, proj, proj, proj, proj)


def _ret_kernel(lgf_ref, lgb_ref, gain_ref, q_ref, k_ref, v_ref, g_ref, o_ref, st_ref, *, n_chunks):
    h = pl.program_id(1)
    C = RET_CHUNK
    lgf = lgf_ref[h]
    lgb = lgb_ref[h]

    ri = lax.broadcasted_iota(jnp.int32, (C, C), 0).astype(F32)
    ci = lax.broadcasted_iota(jnp.int32, (C, C), 1).astype(F32)
    diff = ri - ci
    dmat = jnp.where(diff >= 0.0,
                     jnp.exp(lgf * jnp.maximum(diff, 0.0)),
                     jnp.exp(lgb * jnp.maximum(-diff, 0.0))) * QK_SCALE
    pos = lax.broadcasted_iota(jnp.int32, (C, HEAD_DIM), 0).astype(F32)
    q_dec_f = jnp.exp(lgf * (pos + 1.0))
    q_dec_b = jnp.exp(lgb * (C - pos))
    k_dec_f = jnp.exp(lgf * (C - 1.0 - pos)) * QK_SCALE
    k_dec_b = jnp.exp(lgb * pos) * QK_SCALE
    chunk_dec_f = jnp.exp(jnp.full((HEAD_DIM, HEAD_DIM), lgf * C, F32))
    chunk_dec_b = jnp.exp(jnp.full((HEAD_DIM, HEAD_DIM), lgb * C, F32))

    def kv_outer(kc, vc, dec):
        kd = (kc.astype(F32) * dec).astype(BF16)
        return lax.dot_general(kd, vc, (((0,), (0,)), ((), ())), preferred_element_type=F32)

    def scan_body(n, carry):
        sf, sb = carry
        m = n_chunks - 1 - n
        st_ref[n, 0:HEAD_DIM, :] = sf.astype(BF16)
        st_ref[m, HEAD_DIM:, :] = sb.astype(BF16)
        rf = pl.ds(pl.multiple_of(n * C, C), C)
        rb = pl.ds(pl.multiple_of(m * C, C), C)
        sf = chunk_dec_f * sf + kv_outer(k_ref[0, rf, :], v_ref[0, rf, :], k_dec_f)
        sb = chunk_dec_b * sb + kv_outer(k_ref[0, rb, :], v_ref[0, rb, :], k_dec_b)
        return sf, sb

    zero = jnp.zeros((HEAD_DIM, HEAD_DIM), F32)
    lax.fori_loop(0, n_chunks, scan_body, (zero, zero), unroll=RET_UNROLL)

    gain = gain_ref[0]

    def out_body(n, carry):
        r = pl.ds(pl.multiple_of(n * C, C), C)
        qc = q_ref[0, r, :]
        kc = k_ref[0, r, :]
        vc = v_ref[0, r, :]
        qf = qc.astype(F32)
        s = lax.dot_general(qc, kc, (((1,), (1,)), ((), ())), preferred_element_type=F32) * dmat
        qcat = jnp.concatenate([(qf * q_dec_f).astype(BF16), (qf * q_dec_b).astype(BF16)], axis=-1)
        o = (jnp.dot(s.astype(BF16), vc, preferred_element_type=F32)
             + jnp.dot(qcat, st_ref[n], preferred_element_type=F32))
        mu = jnp.mean(o, axis=-1, keepdims=True)
        oc = o - mu
        var = jnp.mean(oc * oc, axis=-1, keepdims=True)
        on = oc * lax.rsqrt(var + GN_EPS) * gain
        gt = g_ref[0, r, :].astype(F32)
        o_ref[r, :] = (gt / (1.0 + jnp.exp(-gt)) * on).astype(BF16)
        return carry

    lax.fori_loop(0, n_chunks, out_body, 0, unroll=RET_UNROLL)


def _ret(proj, lgf, lgb, gn_gain, batch, seq):
    rows = batch * seq
    n_chunks = seq // RET_CHUNK

    def slab(slab0):
        return pl.BlockSpec((1, seq, HEAD_DIM), lambda b, h: (slab0 + h, b, 0))

    return pl.pallas_call(
        functools.partial(_ret_kernel, n_chunks=n_chunks),
        grid=(batch, RET_HEADS),
        in_specs=[
            pl.BlockSpec(memory_space=pltpu.SMEM),
            pl.BlockSpec(memory_space=pltpu.SMEM),
            pl.BlockSpec((1, 1, HEAD_DIM), lambda b, h: (h, 0, 0)),
            slab(SLAB_QR), slab(SLAB_KR), slab(SLAB_VR), slab(SLAB_GR),
        ],
        out_specs=pl.BlockSpec((seq, HEAD_DIM), lambda b, h: (b, h)),
        out_shape=jax.ShapeDtypeStruct((rows, RET_WIDTH), BF16),
        scratch_shapes=[pltpu.VMEM((n_chunks, 2 * HEAD_DIM, HEAD_DIM), BF16)],
        compiler_params=_params(("arbitrary", "arbitrary")),
        name="ret",
    )(lgf, lgb, gn_gain.reshape(RET_HEADS, 1, HEAD_DIM), proj, proj, proj, proj)


def _mix_kernel(a_ref, r_ref, x_ref, w_ref, g_ref, b_ref, o_ref):
    for r in range(MIX_BM // MIX_SUB):
        rows = slice(r * MIX_SUB, (r + 1) * MIX_SUB)
        mix = (jnp.dot(a_ref[rows, :], w_ref[0:ATTN_WIDTH, :], preferred_element_type=F32)
               + jnp.dot(r_ref[rows, :], w_ref[ATTN_WIDTH:, :], preferred_element_type=F32))
        z = DEEPNORM_ALPHA * x_ref[rows, :] + mix
        o_ref[rows, :] = _layer_norm_rows(z, g_ref[...], b_ref[...])


def _mix_ln(attn, ret, x2, w_out, gain, bias):
    rows = x2.shape[0]
    return pl.pallas_call(
        _mix_kernel,
        grid=(rows // MIX_BM,),
        in_specs=[
            pl.BlockSpec((MIX_BM, ATTN_WIDTH), lambda i: (i, 0)),
            pl.BlockSpec((MIX_BM, RET_WIDTH), lambda i: (i, 0)),
            pl.BlockSpec((MIX_BM, D_MODEL), lambda i: (i, 0)),
            pl.BlockSpec((D_MODEL, D_MODEL), lambda i: (0, 0), pipeline_mode=pl.Buffered(1)),
            pl.BlockSpec((1, D_MODEL), lambda i: (0, 0)),
            pl.BlockSpec((1, D_MODEL), lambda i: (0, 0)),
        ],
        out_specs=pl.BlockSpec((MIX_BM, D_MODEL), lambda i: (i, 0)),
        out_shape=jax.ShapeDtypeStruct((rows, D_MODEL), F32),
        compiler_params=_params(("arbitrary",)),
        name="mix_ln",
    )(attn, ret, x2, w_out, gain, bias)


def _ffn_kernel(h_ref, wg_ref, wu_ref, wo_ref, g_ref, b_ref, o_ref, hb_ref):
    j = pl.program_id(1)

    @pl.when(j == 0)
    def _():
        h = h_ref[...]
        hb_ref[...] = h.astype(BF16)
        o_ref[...] = DEEPNORM_ALPHA * h

    hb = hb_ref[...]
    gate = jnp.dot(hb, wg_ref[...], preferred_element_type=F32)
    up = jnp.dot(hb, wu_ref[...], preferred_element_type=F32)
    act = (gate / (1.0 + jnp.exp(-gate)) * up).astype(BF16)
    for c in range(D_MODEL // FFN_NC):
        cols = slice(c * FFN_NC, (c + 1) * FFN_NC)
        o_ref[:, cols] += jnp.dot(act, wo_ref[:, cols], preferred_element_type=F32)

    @pl.when(j == pl.num_programs(1) - 1)
    def _():
        o_ref[...] = _layer_norm_rows(o_ref[...], g_ref[...], b_ref[...])


def _ffn_ln(h, w_ffn_in, w_ffn_out, gain, bias):
    rows = h.shape[0]
    nh = FFN_HIDDEN // FFN_BH
    return pl.pallas_call(
        _ffn_kernel,
        grid=(rows // FFN_BM, nh),
        in_specs=[
            pl.BlockSpec((FFN_BM, D_MODEL), lambda i, j: (i, 0)),
            pl.BlockSpec((D_MODEL, FFN_BH), lambda i, j: (0, j)),
            pl.BlockSpec((D_MODEL, FFN_BH), lambda i, j: (0, nh + j)),
            pl.BlockSpec((FFN_BH, D_MODEL), lambda i, j: (j, 0)),
            pl.BlockSpec((1, D_MODEL), lambda i, j: (0, 0)),
            pl.BlockSpec((1, D_MODEL), lambda i, j: (0, 0)),
        ],
        out_specs=pl.BlockSpec((FFN_BM, D_MODEL), lambda i, j: (i, 0)),
        out_shape=jax.ShapeDtypeStruct((rows, D_MODEL), F32),
        scratch_shapes=[pltpu.VMEM((FFN_BM, D_MODEL), BF16)],
        compiler_params=_params(("arbitrary", "arbitrary")),
        name="ffn_ln",
    )(h, w_ffn_in, w_ffn_in, w_ffn_out, gain, bias)


def _layer(x, w_in, sink, lgf, lgb, gn_gain, w_out, ln1_g, ln1_b, w_ffn_in, w_ffn_out, ln2_g, ln2_b):
    batch, seq, _ = x.shape
    x2 = x.reshape(batch * seq, D_MODEL)
    proj = _proj(x2, w_in)
    attn = _attn(proj, sink, batch, seq)
    ret = _ret(proj, lgf, lgb, gn_gain, batch, seq)
    h = _mix_ln(attn, ret, x2, w_out, ln1_g, ln1_b)
    y = _ffn_ln(h, w_ffn_in, w_ffn_out, ln2_g, ln2_b)
    return y.reshape(batch, seq, D_MODEL)


def kernel(x_prompt, x_sample, w_in, attn_sink, ret_decay_fwd, ret_decay_bwd, ret_gn_gain, w_out,
           ln1_gain, ln1_bias, w_ffn_in, w_ffn_out, ln2_gain, ln2_bias):
    l = 0
    args = (
        w_in[l].astype(BF16),
        attn_sink[l].astype(F32),
        jax.nn.log_sigmoid(ret_decay_fwd[l].astype(F32)),
        jax.nn.log_sigmoid(ret_decay_bwd[l].astype(F32)),
        ret_gn_gain[l].astype(F32),
        w_out[l].astype(BF16),
        ln1_gain[l].astype(F32).reshape(1, D_MODEL),
        ln1_bias[l].astype(F32).reshape(1, D_MODEL),
        w_ffn_in[l].astype(BF16),
        w_ffn_out[l].astype(BF16),
        ln2_gain[l].astype(F32).reshape(1, D_MODEL),
        ln2_bias[l].astype(F32).reshape(1, D_MODEL),
    )
    return (_layer(x_prompt, *args), _layer(x_sample, *args))
```

```python
import functools

import jax
import jax.numpy as jnp
from jax import lax
from jax.experimental import pallas as pl
from jax.experimental.pallas import tpu as pltpu

F32 = jnp.float32
BF16 = jnp.bfloat16

D_MODEL = 2048
HEAD_DIM = 128
ATTN_Q_HEADS = 8
ATTN_KV_HEADS = 2
ATTN_GROUP = ATTN_Q_HEADS // ATTN_KV_HEADS
ATTN_WIDTH = ATTN_Q_HEADS * HEAD_DIM
RET_HEADS = 8
RET_WIDTH = RET_HEADS * HEAD_DIM
IN_WIDTH = 5632
FFN_HIDDEN = 5632
WINDOW = 128
DEEPNORM_ALPHA = 2.0 ** 0.25
LN_EPS = 1e-5
GN_EPS = 1e-5
NEG_INF = -1e30
QK_SCALE = HEAD_DIM ** -0.5

N_SLABS = IN_WIDTH // HEAD_DIM
SLAB_QA = 0
SLAB_KA = SLAB_QA + ATTN_Q_HEADS
SLAB_VA = SLAB_KA + ATTN_KV_HEADS
SLAB_QR = SLAB_VA + ATTN_KV_HEADS
SLAB_KR = SLAB_QR + RET_HEADS
SLAB_VR = SLAB_KR + RET_HEADS
SLAB_GR = SLAB_VR + RET_HEADS

VMEM_LIMIT = 60 * 1024 * 1024

PROJ_BM = 512
PROJ_NC = 512
ATTN_TQ = 512
RET_CHUNK = 256
RET_UNROLL = 4
RET_OUT_UNROLL = 8
MIX_BM = 512
MIX_SUB = 256
FFN_BM = 1024
FFN_BH = 512
FFN_NC = 512
FFN_LN_ROWS = 128


def _params(sem):
    return pltpu.CompilerParams(dimension_semantics=sem, vmem_limit_bytes=VMEM_LIMIT)


def _layer_norm_rows(z, gain, bias):
    mu = jnp.mean(z, axis=-1, keepdims=True)
    zc = z - mu
    var = jnp.mean(zc * zc, axis=-1, keepdims=True)
    return zc * lax.rsqrt(var + LN_EPS) * gain + bias


def _proj_kernel(x_ref, w_ref, o_ref):
    xb = x_ref[...].astype(BF16)
    slabs_per_chunk = PROJ_NC // HEAD_DIM
    for c in range(IN_WIDTH // PROJ_NC):
        acc = jnp.dot(xb, w_ref[:, c * PROJ_NC:(c + 1) * PROJ_NC], preferred_element_type=F32)
        for s in range(slabs_per_chunk):
            o_ref[c * slabs_per_chunk + s] = acc[:, s * HEAD_DIM:(s + 1) * HEAD_DIM].astype(BF16)


def _proj(x2, w_in):
    rows = x2.shape[0]
    return pl.pallas_call(
        _proj_kernel,
        grid=(rows // PROJ_BM,),
        in_specs=[
            pl.BlockSpec((PROJ_BM, D_MODEL), lambda i: (i, 0)),
            pl.BlockSpec((D_MODEL, IN_WIDTH), lambda i: (0, 0), pipeline_mode=pl.Buffered(1)),
        ],
        out_specs=pl.BlockSpec((N_SLABS, PROJ_BM, HEAD_DIM), lambda i: (0, i, 0)),
        out_shape=jax.ShapeDtypeStruct((N_SLABS, rows, HEAD_DIM), BF16),
        compiler_params=_params(("arbitrary",)),
        name="proj",
    )(x2, w_in)


def _attn_kernel(sink_ref, q_ref, kp_ref, kc_ref, kn_ref, vp_ref, vc_ref, vn_ref, o_ref,
                 bias_ref, kfull_ref, vfull_ref, *, n_tiles):
    b, kv, t = pl.program_id(0), pl.program_id(1), pl.program_id(2)
    W = WINDOW
    nsub = ATTN_TQ // W

    @pl.when((b == 0) & (kv == 0) & (t == 0))
    def _():
        qi = lax.broadcasted_iota(jnp.int32, (W, 3 * W), 0)
        kj = lax.broadcasted_iota(jnp.int32, (W, 3 * W), 1) - W
        dist = jnp.abs(qi - kj)
        distf = dist.astype(F32)
        inwin = dist <= W
        for h in range(ATTN_Q_HEADS):
            slope = 2.0 ** (-8.0 * (h + 1) / ATTN_Q_HEADS)
            base = jnp.where(inwin, -slope * distf, NEG_INF)
            bias_ref[0, h] = base
            bias_ref[1, h] = jnp.where(kj >= 0, base, NEG_INF)
            bias_ref[2, h] = jnp.where(kj < W, base, NEG_INF)

    kfull_ref[0:W] = kp_ref[0]
    kfull_ref[W:W + ATTN_TQ] = kc_ref[0]
    kfull_ref[W + ATTN_TQ:] = kn_ref[0]
    vfull_ref[0:W] = vp_ref[0]
    vfull_ref[W:W + ATTN_TQ] = vc_ref[0]
    vfull_ref[W + ATTN_TQ:] = vn_ref[0]

    for i in range(nsub):
        if i == 0:
            variant = jnp.where(t == 0, 1, 0)
        elif i == nsub - 1:
            variant = jnp.where(t == n_tiles - 1, 2, 0)
        else:
            variant = 0
        q4 = q_ref[:, i * W:(i + 1) * W, :].reshape(ATTN_GROUP * W, HEAD_DIM)
        kb = kfull_ref[i * W:(i + 3) * W]
        vb = vfull_ref[i * W:(i + 3) * W]
        s = lax.dot_general(q4, kb, (((1,), (1,)), ((), ())), preferred_element_type=F32)
        for g in range(ATTN_GROUP):
            h = kv * ATTN_GROUP + g
            sink = sink_ref[h]
            sg = s[g * W:(g + 1) * W] * QK_SCALE + bias_ref[variant, h]
            m = jnp.maximum(jnp.max(sg, axis=-1, keepdims=True), sink)
            e = jnp.exp(sg - m)
            den = jnp.sum(e, axis=-1, keepdims=True) + jnp.exp(sink - m)
            og = jnp.dot(e.astype(BF16), vb, preferred_element_type=F32) / den
            o_ref[i * W:(i + 1) * W, g * HEAD_DIM:(g + 1) * HEAD_DIM] = og.astype(BF16)


def _attn(proj, sink, batch, seq):
    rows = batch * seq
    n_tiles = seq // ATTN_TQ
    nblk = seq // WINDOW
    sub = ATTN_TQ // WINDOW
    assert nblk >= 2 and sub >= 2

    def cur(slab0):
        return pl.BlockSpec((1, ATTN_TQ, HEAD_DIM), lambda b, k, t: (slab0 + k, b * n_tiles + t, 0))

    def prev(slab0):
        return pl.BlockSpec((1, WINDOW, HEAD_DIM),
                            lambda b, k, t: (slab0 + k, b * nblk + jnp.maximum(t * sub - 1, 0), 0))

    def nxt(slab0):
        return pl.BlockSpec((1, WINDOW, HEAD_DIM),
                            lambda b, k, t: (slab0 + k, b * nblk + jnp.minimum((t + 1) * sub, nblk - 1), 0))

    return pl.pallas_call(
        functools.partial(_attn_kernel, n_tiles=n_tiles),
        grid=(batch, ATTN_KV_HEADS, n_tiles),
        in_specs=[
            pl.BlockSpec(memory_space=pltpu.SMEM),
            pl.BlockSpec((ATTN_GROUP, ATTN_TQ, HEAD_DIM), lambda b, k, t: (k, b * n_tiles + t, 0)),
            prev(SLAB_KA), cur(SLAB_KA), nxt(SLAB_KA),
            prev(SLAB_VA), cur(SLAB_VA), nxt(SLAB_VA),
        ],
        out_specs=pl.BlockSpec((ATTN_TQ, ATTN_GROUP * HEAD_DIM), lambda b, k, t: (b * n_tiles + t, k)),
        out_shape=jax.ShapeDtypeStruct((rows, ATTN_WIDTH), BF16),
        scratch_shapes=[
            pltpu.VMEM((3, ATTN_Q_HEADS, WINDOW, 3 * WINDOW), F32),
            pltpu.VMEM((ATTN_TQ + 2 * WINDOW, HEAD_DIM), BF16),
            pltpu.VMEM((ATTN_TQ + 2 * WINDOW, HEAD_DIM), BF16),
        ],
        compiler_params=_params(("arbitrary", "arbitrary", "arbitrary")),
        name="attn",
    )(sink, proj, proj, proj, proj, proj, proj, proj)


def _ret_kernel(lgf_ref, lgb_ref, gain_ref, q_ref, k_ref, v_ref, g_ref, o_ref,
                dmat_ref, qdec_ref, kdec_ref, st_ref, p_ref, gate_ref, *, n_chunks):
    h, b = pl.program_id(0), pl.program_id(1)
    C = RET_CHUNK
    HD = HEAD_DIM
    lgf = lgf_ref[h]
    lgb = lgb_ref[h]

    @pl.when(b == 0)
    def _():
        ri = lax.broadcasted_iota(jnp.int32, (C, C), 0).astype(F32)
        ci = lax.broadcasted_iota(jnp.int32, (C, C), 1).astype(F32)
        diff = ri - ci
        dmat_ref[...] = jnp.where(diff >= 0.0,
                                  jnp.exp(lgf * jnp.maximum(diff, 0.0)),
                                  jnp.exp(lgb * jnp.maximum(-diff, 0.0))) * QK_SCALE
        pos = lax.broadcasted_iota(jnp.int32, (C, HD), 0).astype(F32)
        qdec_ref[0] = jnp.exp(lgf * (pos + 1.0))
        qdec_ref[1] = jnp.exp(lgb * (C - pos))
        kdec_ref[0] = (jnp.exp(lgf * (C - 1.0 - pos)) * QK_SCALE).astype(BF16)
        kdec_ref[1] = (jnp.exp(lgb * pos) * QK_SCALE).astype(BF16)

    chunk_dec_f = jnp.exp(jnp.full((HD, HD), lgf * C, F32))
    chunk_dec_b = jnp.exp(jnp.full((HD, HD), lgb * C, F32))

    def rows_of(n):
        return pl.ds(pl.multiple_of(n * C, C), C)

    def kv_outer(kd, vc):
        return lax.dot_general(kd, vc, (((0,), (0,)), ((), ())), preferred_element_type=F32)

    gain = gain_ref[0]

    def scan_body(n, carry):
        sf, sb = carry
        m = n_chunks - 1 - n
        r = rows_of(n)
        vf = v_ref[0, r, :]
        kc = k_ref[0, r, :]
        st_ref[n, :, 0:HD] = sf.astype(BF16)
        st_ref[m, :, HD:] = sb.astype(BF16)
        sf = chunk_dec_f * sf + kv_outer(kc * kdec_ref[0], vf)
        sb = chunk_dec_b * sb + kv_outer(k_ref[0, rows_of(m), :] * kdec_ref[1], v_ref[0, rows_of(m), :])
        s = lax.dot_general(q_ref[0, r, :], kc, (((1,), (1,)), ((), ())), preferred_element_type=F32)
        p_ref[n] = (s * dmat_ref[...]).astype(BF16)
        gt = g_ref[0, r, :].astype(F32)
        gate_ref[r, :] = gt / (1.0 + jnp.exp(-gt)) * gain
        return sf, sb

    zero = jnp.zeros((HD, HD), F32)
    lax.fori_loop(0, n_chunks, scan_body, (zero, zero), unroll=RET_UNROLL)

    def out_body(n, carry):
        r = rows_of(n)
        cross = jnp.dot(q_ref[0, r, :], st_ref[n], preferred_element_type=F32)
        o = (jnp.dot(p_ref[n], v_ref[0, r, :], preferred_element_type=F32)
             + qdec_ref[0] * cross[:, 0:HD] + qdec_ref[1] * cross[:, HD:])
        mu = jnp.mean(o, axis=-1, keepdims=True)
        oc = o - mu
        var = jnp.mean(oc * oc, axis=-1, keepdims=True)
        o_ref[r, :] = (oc * lax.rsqrt(var + GN_EPS) * gate_ref[r, :]).astype(BF16)
        return carry

    lax.fori_loop(0, n_chunks, out_body, 0, unroll=RET_OUT_UNROLL)


def _ret(proj, lgf, lgb, gn_gain, batch, seq):
    rows = batch * seq
    n_chunks = seq // RET_CHUNK
    C = RET_CHUNK

    def slab(slab0):
        return pl.BlockSpec((1, seq, HEAD_DIM), lambda h, b: (slab0 + h, b, 0))

    return pl.pallas_call(
        functools.partial(_ret_kernel, n_chunks=n_chunks),
        grid=(RET_HEADS, batch),
        in_specs=[
            pl.BlockSpec(memory_space=pltpu.SMEM),
            pl.BlockSpec(memory_space=pltpu.SMEM),
            pl.BlockSpec((1, 1, HEAD_DIM), lambda h, b: (h, 0, 0)),
            slab(SLAB_QR), slab(SLAB_KR), slab(SLAB_VR), slab(SLAB_GR),
        ],
        out_specs=pl.BlockSpec((seq, HEAD_DIM), lambda h, b: (b, h)),
        out_shape=jax.ShapeDtypeStruct((rows, RET_WIDTH), BF16),
        scratch_shapes=[
            pltpu.VMEM((C, C), F32),
            pltpu.VMEM((2, C, HEAD_DIM), F32),
            pltpu.VMEM((2, C, HEAD_DIM), BF16),
            pltpu.VMEM((n_chunks, HEAD_DIM, 2 * HEAD_DIM), BF16),
            pltpu.VMEM((n_chunks, C, C), BF16),
            pltpu.VMEM((seq, HEAD_DIM), F32),
        ],
        compiler_params=_params(("arbitrary", "arbitrary")),
        name="ret",
    )(lgf, lgb, gn_gain.reshape(RET_HEADS, 1, HEAD_DIM), proj, proj, proj, proj)


def _mix_kernel(a_ref, r_ref, x_ref, w_ref, g_ref, b_ref, o_ref, ob_ref):
    for r in range(MIX_BM // MIX_SUB):
        rows = slice(r * MIX_SUB, (r + 1) * MIX_SUB)
        mix = (jnp.dot(a_ref[rows, :], w_ref[0:ATTN_WIDTH, :], preferred_element_type=F32)
               + jnp.dot(r_ref[rows, :], w_ref[ATTN_WIDTH:, :], preferred_element_type=F32))
        z = DEEPNORM_ALPHA * x_ref[rows, :] + mix
        h = _layer_norm_rows(z, g_ref[...], b_ref[...])
        o_ref[rows, :] = h
        ob_ref[rows, :] = h.astype(BF16)


def _mix_ln(attn, ret, x2, w_out, gain, bias):
    rows = x2.shape[0]
    return pl.pallas_call(
        _mix_kernel,
        grid=(rows // MIX_BM,),
        in_specs=[
            pl.BlockSpec((MIX_BM, ATTN_WIDTH), lambda i: (i, 0)),
            pl.BlockSpec((MIX_BM, RET_WIDTH), lambda i: (i, 0)),
            pl.BlockSpec((MIX_BM, D_MODEL), lambda i: (i, 0)),
            pl.BlockSpec((D_MODEL, D_MODEL), lambda i: (0, 0), pipeline_mode=pl.Buffered(1)),
            pl.BlockSpec((1, D_MODEL), lambda i: (0, 0)),
            pl.BlockSpec((1, D_MODEL), lambda i: (0, 0)),
        ],
        out_specs=[pl.BlockSpec((MIX_BM, D_MODEL), lambda i: (i, 0)),
                   pl.BlockSpec((MIX_BM, D_MODEL), lambda i: (i, 0))],
        out_shape=[jax.ShapeDtypeStruct((rows, D_MODEL), F32),
                   jax.ShapeDtypeStruct((rows, D_MODEL), BF16)],
        compiler_params=_params(("arbitrary",)),
        name="mix_ln",
    )(attn, ret, x2, w_out, gain, bias)


def _ffn_kernel(hb_ref, hres_ref, wg_ref, wu_ref, wo_ref, g_ref, b_ref, o_ref, acc_ref, *, n_blocks):
    i, j = pl.program_id(0), pl.program_id(1)
    slot = i % 2

    @pl.when((i == 0) & (j == 0))
    def _():
        acc_ref[...] = jnp.zeros(acc_ref.shape, F32)

    def matmul_part():
        hb = hb_ref[...]
        gate = jnp.dot(hb, wg_ref[...], preferred_element_type=F32)
        up = jnp.dot(hb, wu_ref[...], preferred_element_type=F32)
        act = (gate / (1.0 + jnp.exp(-gate)) * up).astype(BF16)
        for c in range(D_MODEL // FFN_NC):
            cols = slice(c * FFN_NC, (c + 1) * FFN_NC)
            acc_ref[slot, :, cols] += jnp.dot(act, wo_ref[:, cols], preferred_element_type=F32)

    def norm_part():
        rows = pl.ds(pl.multiple_of(j * FFN_LN_ROWS, FFN_LN_ROWS), FFN_LN_ROWS)
        z = DEEPNORM_ALPHA * hres_ref[...] + acc_ref[1 - slot, rows, :]
        o_ref[...] = _layer_norm_rows(z, g_ref[...], b_ref[...])
        acc_ref[1 - slot, rows, :] = jnp.zeros((FFN_LN_ROWS, D_MODEL), F32)

    has_mm = i < n_blocks
    has_ln = (i > 0) & (j < FFN_BM // FFN_LN_ROWS)

    @pl.when(has_mm & has_ln)
    def _():
        norm_part()
        matmul_part()

    @pl.when(has_mm & jnp.logical_not(has_ln))
    def _():
        matmul_part()

    @pl.when(jnp.logical_not(has_mm) & has_ln)
    def _():
        norm_part()


def _ffn_ln(hb, h, w_ffn_in, w_ffn_out, gain, bias):
    rows = h.shape[0]
    nh = FFN_HIDDEN // FFN_BH
    n_blocks = rows // FFN_BM
    n_ln = FFN_BM // FFN_LN_ROWS
    assert n_ln <= nh

    def ln_slice(i, j):
        return (jnp.where(i == 0, 0, (i - 1) * n_ln + jnp.minimum(j, n_ln - 1)), 0)

    def hidden_chunk(i, j):
        return jnp.where(i == n_blocks, nh - 1, j)

    return pl.pallas_call(
        functools.partial(_ffn_kernel, n_blocks=n_blocks),
        grid=(n_blocks + 1, nh),
        in_specs=[
            pl.BlockSpec((FFN_BM, D_MODEL), lambda i, j: (jnp.minimum(i, n_blocks - 1), 0)),
            pl.BlockSpec((FFN_LN_ROWS, D_MODEL), ln_slice),
            pl.BlockSpec((D_MODEL, FFN_BH), lambda i, j: (0, hidden_chunk(i, j))),
            pl.BlockSpec((D_MODEL, FFN_BH), lambda i, j: (0, nh + hidden_chunk(i, j))),
            pl.BlockSpec((FFN_BH, D_MODEL), lambda i, j: (hidden_chunk(i, j), 0)),
            pl.BlockSpec((1, D_MODEL), lambda i, j: (0, 0)),
            pl.BlockSpec((1, D_MODEL), lambda i, j: (0, 0)),
        ],
        out_specs=pl.BlockSpec((FFN_LN_ROWS, D_MODEL), ln_slice),
        out_shape=jax.ShapeDtypeStruct((rows, D_MODEL), F32),
        scratch_shapes=[pltpu.VMEM((2, FFN_BM, D_MODEL), F32)],
        compiler_params=_params(("arbitrary", "arbitrary")),
        name="ffn_ln",
    )(hb, h, w_ffn_in, w_ffn_in, w_ffn_out, gain, bias)


def _layer(x, w_in, sink, lgf, lgb, gn_gain, w_out, ln1_g, ln1_b, w_ffn_in, w_ffn_out, ln2_g, ln2_b):
    batch, seq, _ = x.shape
    x2 = x.reshape(batch * seq, D_MODEL)
    proj = _proj(x2, w_in)
    attn = _attn(proj, sink, batch, seq)
    ret = _ret(proj, lgf, lgb, gn_gain, batch, seq)
    h, hb = _mix_ln(attn, ret, x2, w_out, ln1_g, ln1_b)
    y = _ffn_ln(hb, h, w_ffn_in, w_ffn_out, ln2_g, ln2_b)
    return y.reshape(batch, seq, D_MODEL)


def kernel(x_prompt, x_sample, w_in, attn_sink, ret_decay_fwd, ret_decay_bwd, ret_gn_gain, w_out,
           ln1_gain, ln1_bias, w_ffn_in, w_ffn_out, ln2_gain, ln2_bias):
    l = 0
    args = (
        w_in[l].astype(BF16),
        attn_sink[l].astype(F32),
        jax.nn.log_sigmoid(ret_decay_fwd[l].astype(F32)),
        jax.nn.log_sigmoid(ret_decay_bwd[l].astype(F32)),
        ret_gn_gain[l].astype(F32),
        w_out[l].astype(BF16),
        ln1_gain[l].astype(F32).reshape(1, D_MODEL),
        ln1_bias[l].astype(F32).reshape(1, D_MODEL),
        w_ffn_in[l].astype(BF16),
        w_ffn_out[l].astype(BF16),
        ln2_gain[l].astype(F32).reshape(1, D_MODEL),
        ln2_bias[l].astype(F32).reshape(1, D_MODEL),
    )
    return (_layer(x_prompt, *args), _layer(x_sample, *args))
```
